```python
import math, functools
import jax, jax.numpy as jnp
from jax import lax
import numpy as np

D_MODEL = 1024
BATCH = 2
SEQ = 8192
DEPTH = 2
DEC_BATCH = 32
DEC_SEQ = 8
PAST_LEN = 16384
PAGE_SIZE = 128

N_HEADS = 8
HEAD_DIM = 64
ATT_WIDTH = N_HEADS * HEAD_DIM
IDX_HEADS = 8
IDX_DIM = 64
TOPK_MAX = 256
Q_BLOCK = 128
GMLP_GROUPS = 8
GMLP_WIDTH = D_MODEL - ATT_WIDTH
GMLP_GROUP_DIM = GMLP_WIDTH // GMLP_GROUPS
CHUNK = 128
N_BUCKETS = 32
MAX_DISTANCE = 128
PEER_HEADS = 8
PEER_TOPK = 16
N_KEYS = 128
N_EXPERTS = N_KEYS * N_KEYS
D_KEY = 256
PEER_BLOCK = 128
IN_WIDTH = 3 * ATT_WIDTH + IDX_HEADS * IDX_DIM + IDX_DIM + IDX_HEADS + 2 * GMLP_WIDTH
DEEPNORM_ALPHA = (2 * DEPTH) ** 0.25
DEEPNORM_BETA = (8 * DEPTH) ** -0.25
LN_EPS = 1e-5
NEG_INF = -1e30

kernel_name = 'hybrid_dsa_gmlp_peer_decoder_step'


def layer_norm(x, gain=None, bias=None):
    xf = x.astype(jnp.float32)
    xc = xf - jnp.mean(xf, -1, keepdims=True)
    y = xc * lax.rsqrt(jnp.mean(xc * xc, -1, keepdims=True) + LN_EPS)
    if gain is not None:
        y = y * gain.astype(jnp.float32) + bias.astype(jnp.float32)
    return y.astype(x.dtype)


def modulate(x, shift, scale):
    return layer_norm(x) * (1 + scale) + shift


def rel_bucket(dist):
    n = jnp.maximum(dist, 0)
    max_exact = N_BUCKETS // 2
    nf = jnp.maximum(n, 1).astype(jnp.float32)
    large = max_exact + (jnp.log(nf / max_exact) / math.log(MAX_DISTANCE / max_exact)
                         * (N_BUCKETS - max_exact)).astype(jnp.int32)
    return jnp.where(n < max_exact, n, jnp.minimum(large, N_BUCKETS - 1))


batched_rows = jax.vmap(lambda a, i: a[i])


def indexer_scores(qi, wi, ki):
    dots = jnp.einsum('bthd,bsd->bths', qi, ki).astype(jnp.float32) * IDX_DIM ** -0.5
    return jnp.einsum('bths,bth->bts', jax.nn.relu(dots), wi.astype(jnp.float32) * IDX_HEADS ** -0.5)


def select_keys(scores, q_pos, n_sel):
    key_pos = jnp.arange(scores.shape[-1])
    scores = jnp.where(key_pos[None, None, :] <= q_pos[None, :, None], scores, NEG_INF)
    return lax.top_k(scores, n_sel)[1]


def sparse_attend(q, k_sel, v_sel, sel_idx, q_pos, rel_bias):
    dist = q_pos[None, :, None] - sel_idx
    bias = jnp.swapaxes(rel_bias[rel_bucket(dist)], -1, -2).astype(jnp.float32)
    logits = jnp.einsum('bthd,btkhd->bthk', q, k_sel).astype(jnp.float32) * HEAD_DIM ** -0.5 + bias
    logits = jnp.where((dist >= 0)[:, :, None, :], logits, NEG_INF)
    probs = jax.nn.softmax(logits, axis=-1).astype(v_sel.dtype)
    return jnp.einsum('bthk,btkhd->bthd', probs, v_sel)


def dsa_prompt(q, k, v, qi, wi, ki, rel_bias):
    B, S = q.shape[:2]
    n_sel = min(TOPK_MAX, S // 4)

    def query_block(t0):
        sl = lambda a: lax.dynamic_slice_in_dim(a, t0, Q_BLOCK, axis=1)
        q_pos = t0 + jnp.arange(Q_BLOCK)
        idx = select_keys(indexer_scores(sl(qi), sl(wi), ki), q_pos, n_sel)
        return sparse_attend(sl(q), batched_rows(k, idx), batched_rows(v, idx), idx, q_pos, rel_bias)

    out = lax.map(query_block, jnp.arange(0, S, Q_BLOCK))
    return jnp.moveaxis(out, 0, 1).reshape(B, S, N_HEADS, HEAD_DIM)


def dsa_sample(q, k, v, qi, wi, ki, cache_k, cache_v, cache_kidx, page_table, layer, rel_bias):
    Bd, Tn = q.shape[:2]
    past = page_table.shape[1] * PAGE_SIZE
    total = past + Tn
    ki_past = cache_kidx[layer, page_table].reshape(Bd, past, IDX_DIM)
    ki_all = jnp.concatenate([ki_past, ki], axis=1)
    q_pos = past + jnp.arange(Tn)
    idx = select_keys(indexer_scores(qi, wi, ki_all), q_pos, min(TOPK_MAX, total // 4))
    in_past = (idx < past)[..., None, None]
    past_idx = jnp.minimum(idx, past - 1)
    phys = page_table[jnp.arange(Bd)[:, None, None], past_idx // PAGE_SIZE]
    off = past_idx % PAGE_SIZE
    new_idx = jnp.clip(idx - past, 0, Tn - 1)
    k_sel = jnp.where(in_past, cache_k[layer, phys, off], batched_rows(k, new_idx))
    v_sel = jnp.where(in_past, cache_v[layer, phys, off], batched_rows(v, new_idx))
    return sparse_attend(q, k_sel, v_sel, idx, q_pos, rel_bias)


def spatial_gate(u, v, w_s, b_s):
    B, T, G, dg = v.shape
    n_c = -(-T // CHUNK)
    vp = jnp.pad(v, ((0, 0), (0, n_c * CHUNK - T), (0, 0), (0, 0))).reshape(B, n_c, CHUNK, G, dg)
    w = w_s * jnp.tril(jnp.ones((CHUNK, CHUNK), w_s.dtype))
    mixed = jnp.einsum('gij,bcjgd->bcigd', w, vp) + jnp.swapaxes(b_s, 0, 1)[:, :, None]
    return u * mixed.reshape(B, n_c * CHUNK, G, dg)[:, :T]


def hybrid_mixer(h, attend, w_in, gmlp_ln_g, gmlp_ln_b, w_s, b_s, w_out):
    B, T, _ = h.shape
    widths = [ATT_WIDTH, ATT_WIDTH, ATT_WIDTH, IDX_HEADS * IDX_DIM, IDX_DIM, IDX_HEADS, GMLP_WIDTH, GMLP_WIDTH]
    q, k, v, qi, ki, wi, u, vg = jnp.split(h @ w_in, np.cumsum(widths)[:-1].tolist(), axis=-1)
    heads = lambda a: a.reshape(B, T, N_HEADS, HEAD_DIM)
    q, k, v = heads(q), heads(k), heads(v)
    qi = qi.reshape(B, T, IDX_HEADS, IDX_DIM)
    att = attend(q, k, v, qi, wi, ki)
    groups = lambda a: a.reshape(B, T, GMLP_GROUPS, GMLP_GROUP_DIM)
    u = groups(jax.nn.gelu(u))
    vg = groups(layer_norm(jax.nn.gelu(vg), gmlp_ln_g, gmlp_ln_b))
    sgu = spatial_gate(u, vg, w_s, b_s)
    merged = jnp.concatenate([att.reshape(B, T, ATT_WIDTH), sgu.reshape(B, T, GMLP_WIDTH)], axis=-1)
    return merged @ w_out, k, v, ki, vg


def peer_ffn(h, wq, subkeys, u_tab, v_tab):
    B, T, D = h.shape
    x = h.reshape(B * T, D)
    n = x.shape[0]
    n_blk = -(-n // PEER_BLOCK)
    xp = jnp.pad(x, ((0, n_blk * PEER_BLOCK - n), (0, 0))).reshape(n_blk, PEER_BLOCK, D)

    def token_block(xb):
        qh = (xb @ wq).reshape(PEER_BLOCK, PEER_HEADS, 2, D_KEY // 2)
        s = jnp.einsum('nhpd,pkd->nhpk', qh, subkeys).astype(jnp.float32)
        top_s, top_i = lax.top_k(s, PEER_TOPK)
        cand_s = (top_s[:, :, 0, :, None] + top_s[:, :, 1, None, :]).reshape(PEER_BLOCK, PEER_HEADS, -1)
        cand_i = (top_i[:, :, 0, :, None] * N_KEYS + top_i[:, :, 1, None, :]).reshape(PEER_BLOCK, PEER_HEADS, -1)
        best_s, best_j = lax.top_k(cand_s, PEER_TOPK)
        exp_i = jnp.take_along_axis(cand_i, best_j, axis=-1)
        gates = jax.nn.softmax(best_s, axis=-1).astype(xb.dtype)
        act = jax.nn.gelu(jnp.einsum('nhkd,nd->nhk', u_tab[exp_i], xb))
        return jnp.einsum('nhk,nhkd->nd', gates * act, v_tab[exp_i])

    out = lax.map(token_block, xp).reshape(-1, D)[:n]
    return out.reshape(B, T, D)


def run_trunk(x, c, attend_fns, w_ada, b_ada, w_in, gmlp_ln_g, gmlp_ln_b, w_s, b_s, w_out,
              ln_mix_g, ln_mix_b, peer_wq, peer_subkeys, peer_u, peer_v, ln_ffn_g, ln_ffn_b):
    ks, vs, kis, vgs = [], [], [], []
    for l in range(DEPTH):
        ada = jax.nn.silu(c) @ w_ada[l] + b_ada[l]
        sh1, sc1, g1, sh2, sc2, g2 = [a[:, None, :] for a in jnp.split(ada, 6, axis=-1)]
        mix, k, v, ki, vg = hybrid_mixer(modulate(x, sh1, sc1), attend_fns[l], w_in[l], gmlp_ln_g[l],
                                         gmlp_ln_b[l], w_s[l], b_s[l], w_out[l])
        x = layer_norm(DEEPNORM_ALPHA * x + g1 * mix, ln_mix_g[l], ln_mix_b[l])
        ffn = peer_ffn(modulate(x, sh2, sc2), peer_wq[l], peer_subkeys[l], peer_u[l], peer_v[l])
        x = layer_norm(DEEPNORM_ALPHA * x + g2 * ffn, ln_ffn_g[l], ln_ffn_b[l])
        ks.append(k); vs.append(v); kis.append(ki); vgs.append(vg)
    return x, jnp.stack(ks), jnp.stack(vs), jnp.stack(kis), jnp.stack(vgs)


def setup_inputs(seed: int = 0) -> dict:
    key = jax.random.key(seed)
    k = jax.random.split(key, 32)
    f32 = jnp.float32
    d = D_MODEL
    n_pages = PAST_LEN // PAGE_SIZE
    n_used = DEC_BATCH * n_pages
    n_pool = (5 * n_used + 3) // 4

    def nrm(kk, shape, scale):
        return jax.random.normal(kk, shape, f32) * scale

    v_col = 2 * ATT_WIDTH
    in_col_scale = jnp.ones((IN_WIDTH,), f32).at[v_col:v_col + ATT_WIDTH].set(DEEPNORM_BETA)
    return {
        'x_prompt': nrm(k[0], (BATCH, SEQ, d), 1.0),
        'x_sample': nrm(k[1], (DEC_BATCH, DEC_SEQ, d), 1.0),
        'cache_k': nrm(k[2], (DEPTH, n_pool, PAGE_SIZE, N_HEADS, HEAD_DIM), 1.0),
        'cache_v': nrm(k[3], (DEPTH, n_pool, PAGE_SIZE, N_HEADS, HEAD_DIM), 1.0),
        'cache_kidx': nrm(k[4], (DEPTH, n_pool, PAGE_SIZE, IDX_DIM), 1.0),
        'page_table': jax.random.permutation(k[5], n_pool)[:n_used].reshape(DEC_BATCH, n_pages).astype(jnp.int32),
        'c_prompt': nrm(k[6], (BATCH, d), 1.0),
        'c_sample': nrm(k[7], (DEC_BATCH, d), 1.0),
        'rel_bias': nrm(k[8], (N_BUCKETS, N_HEADS), 0.3),
        'w_ada': nrm(k[9], (DEPTH, d, 6 * d), d ** -0.5),
        'b_ada': nrm(k[10], (DEPTH, 6 * d), 0.02),
        'w_in': nrm(k[11], (DEPTH, d, IN_WIDTH), d ** -0.5) * in_col_scale,
        'gmlp_ln_g': 1.0 + nrm(k[12], (DEPTH, GMLP_WIDTH), 0.02),
        'gmlp_ln_b': nrm(k[13], (DEPTH, GMLP_WIDTH), 0.02),
        'w_s': nrm(k[14], (DEPTH, GMLP_GROUPS, CHUNK, CHUNK), CHUNK ** -0.5),
        'b_s': 1.0 + nrm(k[15], (DEPTH, GMLP_GROUPS, CHUNK), 0.1),
        'w_out': nrm(k[16], (DEPTH, d, d), d ** -0.5 * DEEPNORM_BETA),
        'ln_mix_g': 1.0 + nrm(k[17], (DEPTH, d), 0.02),
        'ln_mix_b': nrm(k[18], (DEPTH, d), 0.02),
        'peer_wq': nrm(k[19], (DEPTH, d, PEER_HEADS * D_KEY), d ** -0.5),
        'peer_subkeys': nrm(k[20], (DEPTH, 2, N_KEYS, D_KEY // 2), (D_KEY // 2) ** -0.5),
        'peer_u': nrm(k[21], (DEPTH, N_EXPERTS, d), d ** -0.5),
        'peer_v': nrm(k[22], (DEPTH, N_EXPERTS, d), DEEPNORM_BETA),
        'ln_ffn_g': 1.0 + nrm(k[23], (DEPTH, d), 0.02),
        'ln_ffn_b': nrm(k[24], (DEPTH, d), 0.02),
    }


def reference(x_prompt, x_sample, cache_k, cache_v, cache_kidx, page_table, c_prompt, c_sample, rel_bias,
              w_ada, b_ada, w_in, gmlp_ln_g, gmlp_ln_b, w_s, b_s, w_out, ln_mix_g, ln_mix_b,
              peer_wq, peer_subkeys, peer_u, peer_v, ln_ffn_g, ln_ffn_b):
    weights = (w_ada, b_ada, w_in, gmlp_ln_g, gmlp_ln_b, w_s, b_s, w_out, ln_mix_g, ln_mix_b,
               peer_wq, peer_subkeys, peer_u, peer_v, ln_ffn_g, ln_ffn_b)
    prompt_attend = [functools.partial(dsa_prompt, rel_bias=rel_bias) for _ in range(DEPTH)]
    sample_attend = [functools.partial(dsa_sample, cache_k=cache_k, cache_v=cache_v, cache_kidx=cache_kidx,
                                       page_table=page_table, layer=l, rel_bias=rel_bias) for l in range(DEPTH)]
    y_prompt, k_p, v_p, ki_p, _ = run_trunk(x_prompt, c_prompt, prompt_attend, *weights)
    y_sample, k_s, v_s, ki_s, vg_s = run_trunk(x_sample, c_sample, sample_attend, *weights)
    return (y_prompt, y_sample, k_p, v_p, ki_p, k_s, v_s, ki_s, vg_s)
```

```python
import functools
import math

import numpy as np
import jax
import jax.numpy as jnp
from jax import lax
from jax.experimental import pallas as pl
from jax.experimental.pallas import tpu as pltpu

F32 = jnp.float32
BF16 = jnp.bfloat16
I32 = jnp.int32

D_MODEL = 1024
DEPTH = 2
PAGE_SIZE = 128
N_HEADS = 8
HEAD_DIM = 64
ATT_WIDTH = N_HEADS * HEAD_DIM
IDX_HEADS = 8
IDX_DIM = 64
TOPK_MAX = 256
GMLP_GROUPS = 8
GMLP_WIDTH = D_MODEL - ATT_WIDTH
CHUNK = 128
N_BUCKETS = 32
MAX_DISTANCE = 128
PEER_HEADS = 8
PEER_TOPK = 16
N_KEYS = 128
N_EXPERTS = N_KEYS * N_KEYS
D_KEY = 256
DEEPNORM_ALPHA = (2 * DEPTH) ** 0.25
LN_EPS = 1e-5
NEG_INF = -1e30
MASKED = -2e30

LANES = 128
HALF = 64
KEY_CHUNK = 512
VMEM_LIMIT = 56 * 1024 * 1024

INT_MIN = -2 ** 31


def _ordered_key_of(x):
    b = int(np.array(x, np.float32).view(np.int32))
    return b ^ 0x7FFFFFFF if b < 0 else b


KEY_OF_NEG_INF = _ordered_key_of(NEG_INF)


def _ordered_keys(s):
    s = jnp.where(s == 0.0, 0.0, s)
    bits = lax.bitcast_convert_type(s, I32)
    return jnp.where(bits < 0, bits ^ jnp.int32(0x7FFFFFFF), bits)


def _layer_norm(x):
    mu = jnp.mean(x, axis=-1, keepdims=True)
    xc = x - mu
    var = jnp.mean(xc * xc, axis=-1, keepdims=True)
    return xc * lax.rsqrt(var + LN_EPS)


def _dot(a, b):
    return jnp.dot(a, b, preferred_element_type=F32)


def _dot_nt(a, b):
    return lax.dot_general(a, b, (((1,), (1,)), ((), ())), preferred_element_type=F32)


def _split3(a):
    hi = a.astype(BF16)
    lo = (a - hi.astype(F32)).astype(BF16)
    return hi, lo


def _dot3(a, b):
    ah, al = _split3(a)
    bh, bl = _split3(b)
    return _dot(ah, bh) + (_dot(ah, bl) + _dot(al, bh))


def _lane_half_select(shape, lo, hi):
    lane = lax.broadcasted_iota(I32, shape, len(shape) - 1)
    return jnp.where(lane < HALF, lo, hi)


def _params(sem, vmem=VMEM_LIMIT):
    return pltpu.CompilerParams(dimension_semantics=sem, vmem_limit_bytes=vmem)


def _resident(block_shape, index_map):
    return pl.BlockSpec(block_shape, index_map, pipeline_mode=pl.Buffered(1))


def _ada_kernel(c_ref, w_ref, b_ref, o_ref):
    c = c_ref[...]
    s = c * jax.nn.sigmoid(c)
    o_ref[0] = _dot3(s, w_ref[0]) + b_ref[0]


def ada_modulation(c_all, w_ada, b_ada):
    g, d = c_all.shape
    n6 = w_ada.shape[-1]
    tn = 1536
    return pl.pallas_call(
        _ada_kernel,
        grid=(DEPTH, n6 // tn),
        in_specs=[pl.BlockSpec((g, d), lambda l, j: (0, 0)),
                  pl.BlockSpec((1, d, tn), lambda l, j: (l, 0, j)),
                  pl.BlockSpec((1, 1, tn), lambda l, j: (l, 0, j))],
        out_specs=pl.BlockSpec((1, g, tn), lambda l, j: (l, 0, j)),
        out_shape=jax.ShapeDtypeStruct((DEPTH, g, n6), F32),
        compiler_params=_params(("arbitrary", "arbitrary")),
        name="ada_modulation",
    )(c_all, w_ada, b_ada.reshape(DEPTH, 1, n6))


def _rel_bucket(dist):
    n = jnp.maximum(dist, 0)
    max_exact = N_BUCKETS // 2
    nf = jnp.maximum(n, 1).astype(F32)
    large = max_exact + (jnp.log(nf / max_exact) / math.log(MAX_DISTANCE / max_exact)
                         * (N_BUCKETS - max_exact)).astype(I32)
    return jnp.where(n < max_exact, n, jnp.minimum(large, N_BUCKETS - 1))


def _bias_of_bucket(bucket, rb_ref, h):
    out = jnp.zeros(bucket.shape, F32)
    for c in range(N_BUCKETS):
        out = jnp.where(bucket == c, rb_ref[c, h], out)
    return out


def _bias_tables_kernel(rb_ref, tabs_ref, stab_ref):
    i = lax.broadcasted_iota(I32, (LANES, LANES), 0)
    j = lax.broadcasted_iota(I32, (LANES, LANES), 1)
    b0 = _rel_bucket(i - j)
    b1 = _rel_bucket(i - j + LANES)
    for h in range(N_HEADS):
        tabs_ref[h, 0] = jnp.where(i >= j, _bias_of_bucket(b0, rb_ref, h), MASKED)
        tabs_ref[h, 1] = _bias_of_bucket(b1, rb_ref, h)
        tabs_ref[h, 2] = jnp.full((LANES, LANES), rb_ref[N_BUCKETS - 1, h], F32)
        tabs_ref[h, 3] = jnp.full((LANES, LANES), MASKED, F32)
    r = lax.broadcasted_iota(I32, (HALF, LANES), 0)
    jj = lax.broadcasted_iota(I32, (HALF, LANES), 1)
    t = r % 8
    hrow = r // 8
    bl = _rel_bucket(t + LANES - jj)
    bn = _rel_bucket(t - jj)
    far = jnp.zeros((HALF, LANES), F32)
    last = jnp.zeros((HALF, LANES), F32)
    new = jnp.zeros((HALF, LANES), F32)
    for h in range(N_HEADS):
        far = jnp.where(hrow == h, rb_ref[N_BUCKETS - 1, h], far)
        last = jnp.where(hrow == h, _bias_of_bucket(bl, rb_ref, h), last)
        new = jnp.where(hrow == h, _bias_of_bucket(bn, rb_ref, h), new)
    stab_ref[0] = far
    stab_ref[1] = last
    stab_ref[2] = jnp.where(jj <= t, new, MASKED)


def bias_tables(rel_bias):
    return pl.pallas_call(
        _bias_tables_kernel,
        in_specs=[pl.BlockSpec(memory_space=pltpu.SMEM)],
        out_specs=[pl.BlockSpec(memory_space=pltpu.VMEM), pl.BlockSpec(memory_space=pltpu.VMEM)],
        out_shape=[jax.ShapeDtypeStruct((N_HEADS, 4, LANES, LANES), F32),
                   jax.ShapeDtypeStruct((3, HALF, LANES), F32)],
        name="bias_tables",
    )(rel_bias)


def _mixer_in_kernel(x_ref, sh_ref, sc_ref, wq_ref, wk_ref, wv_ref, wqi_ref, wkw_ref, wu_ref, wvg_ref,
                     lng_ref, lnb_ref, ws_ref, bs_ref,
                     qs_ref, k_ref, v_ref, kb_ref, vb_ref, qi_ref, ki_ref, ki2_ref, wis_ref, sgu_ref,
                     vg_ref, *, cs):
    tb = x_ref.shape[0]
    h = _layer_norm(x_ref[...]) * (1.0 + sc_ref[0]) + sh_ref[0]
    hb = h.astype(BF16)
    q = _dot(hb, wq_ref[...])
    qs_ref[...] = (q * HEAD_DIM ** -0.5).astype(BF16)
    k = _dot(hb, wk_ref[...])
    k_ref[...] = k
    kb_ref[...] = k.astype(BF16)
    v = _dot(hb, wv_ref[...])
    v_ref[...] = v
    vb_ref[...] = v.astype(BF16)
    qi_ref[...] = _dot(hb, wqi_ref[...]).astype(BF16)
    kw = _dot(hb, wkw_ref[...])
    ki = kw[:, :IDX_DIM]
    ki_ref[...] = ki
    kib = ki.astype(BF16)
    ki2_ref[...] = jnp.concatenate([kib, kib], axis=1)
    wis_ref[...] = kw[:, IDX_DIM:IDX_DIM + IDX_HEADS] * (IDX_DIM ** -0.5 * IDX_HEADS ** -0.5)
    u = jax.nn.gelu(_dot(hb, wu_ref[...]))
    vg = _layer_norm(jax.nn.gelu(_dot(hb, wvg_ref[...]))) * lng_ref[...] + lnb_ref[...]
    vg_ref[...] = vg
    vgb = vg.astype(BF16)
    for c in range(tb // cs):
        rows = slice(c * cs, (c + 1) * cs)
        for p in range(GMLP_GROUPS // 2):
            cols = slice(p * LANES, (p + 1) * LANES)
            vp = vgb[rows, cols]
            mixed = _lane_half_select((cs, LANES), _dot(ws_ref[2 * p], vp), _dot(ws_ref[2 * p + 1], vp))
            sgu_ref[rows, cols] = (u[rows, cols] * (mixed + bs_ref[p])).astype(BF16)


def mixer_in(x, sh, sc, cond_map, cond_rows, wts, tb, cs):
    n, d = x.shape
    tok = lambda w: pl.BlockSpec((tb, w), lambda i: (i, 0))
    full = lambda a: _resident(a.shape, lambda i: (0,) * a.ndim)
    cond = pl.BlockSpec((1, cond_rows, d), cond_map)
    wnames = ("wq", "wk", "wv", "wqi", "wkw", "wu", "wvg", "lng", "lnb", "ws", "bs")
    warrs = [wts[nm] for nm in wnames]
    out_w = [(ATT_WIDTH, BF16), (ATT_WIDTH, F32), (ATT_WIDTH, F32), (ATT_WIDTH, BF16), (ATT_WIDTH, BF16),
             (IDX_HEADS * IDX_DIM, BF16), (IDX_DIM, F32), (LANES, BF16), (IDX_HEADS, F32),
             (GMLP_WIDTH, BF16), (GMLP_WIDTH, F32)]
    return pl.pallas_call(
        functools.partial(_mixer_in_kernel, cs=cs),
        grid=(n // tb,),
        in_specs=[tok(d), cond, cond] + [full(a) for a in warrs],
        out_specs=[tok(w) for w, _ in out_w],
        out_shape=[jax.ShapeDtypeStruct((n, w), dt) for w, dt in out_w],
        compiler_params=_params(("arbitrary",)),
        name="mixer_in",
    )(x, sh, sc, *warrs)


def _dsa_prompt_kernel(rb_ref, qi_ref, wis_ref, ki2_ref, q_ref, k_ref, v_ref, tabs_ref, tri_ref,
                       o_ref, keys_ref, qiz_ref, qz_ref, wb_ref, m_ref, l_ref, acc_ref, *, seq, n_sel):
    qb = pl.program_id(1)
    QB = LANES
    KC = KEY_CHUNK
    SUB = KC // LANES
    nch = qb // SUB + 1
    n_beyond = seq - nch * KC

    lane = lax.broadcasted_iota(I32, (QB, LANES), 1)
    for p in range(N_HEADS // 2):
        cols = slice(p * LANES, (p + 1) * LANES)
        t = qi_ref[:, cols]
        qiz_ref[p] = jnp.concatenate([jnp.where(lane < HALF, t, 0), jnp.where(lane >= HALF, t, 0)], axis=0)
        t = q_ref[:, cols]
        qz_ref[p] = jnp.concatenate([jnp.where(lane < HALF, t, 0), jnp.where(lane >= HALF, t, 0)], axis=0)
    for h in range(IDX_HEADS):
        wb_ref[h] = jnp.broadcast_to(wis_ref[:, h:h + 1], (QB, LANES))

    row_pos = qb * QB + lax.broadcasted_iota(I32, (QB, KC), 0)
    col = lax.broadcasted_iota(I32, (QB, KC), 1)

    def score_chunk(c, carry):
        kic = ki2_ref[pl.ds(pl.multiple_of(c * KC, KC), KC), :]
        sc = jnp.zeros((QB, KC), F32)
        for p in range(IDX_HEADS // 2):
            d = _dot_nt(qiz_ref[p], kic)
            w0 = jnp.concatenate([wb_ref[2 * p]] * SUB, axis=1)
            w1 = jnp.concatenate([wb_ref[2 * p + 1]] * SUB, axis=1)
            sc = sc + jnp.maximum(d[:QB], 0.0) * w0 + jnp.maximum(d[QB:], 0.0) * w1
        sc = jnp.where(c * KC + col <= row_pos, sc, NEG_INF)
        keys_ref[c] = _ordered_keys(sc)
        return carry

    lax.fori_loop(0, nch, score_chunk, 0)

    def count_ge(cand):
        cb = jnp.broadcast_to(cand, (QB, LANES))

        def body(c, acc):
            blk = keys_ref[c]
            for s in range(SUB):
                acc = acc + jnp.where(blk[:, s * LANES:(s + 1) * LANES] >= cb, 1.0, 0.0)
            return acc

        acc = lax.fori_loop(0, nch, body, jnp.zeros((QB, LANES), F32))
        cnt = jnp.sum(acc, axis=1, keepdims=True)
        return cnt + jnp.where(cand <= KEY_OF_NEG_INF, n_beyond.astype(F32), 0.0)

    kf = float(n_sel)
    t0 = jnp.where(count_ge(jnp.zeros((QB, 1), I32)) >= kf, 0, INT_MIN).astype(I32)

    def bit_step(i, t):
        cand = t + lax.shift_left(jnp.int32(1), 30 - i)
        return jnp.where(count_ge(cand) >= kf, cand, t)

    thr = lax.fori_loop(0, 31, bit_step, t0)
    need = kf - count_ge(thr + 1)

    m_ref[...] = jnp.full(m_ref.shape, NEG_INF, F32)
    l_ref[...] = jnp.zeros(l_ref.shape, F32)
    acc_ref[...] = jnp.zeros(acc_ref.shape, F32)

    def attend_chunk(c, eq_seen, near):
        kbase = pl.multiple_of(c * KC, KC)
        keys = keys_ref[c]
        eq = keys == thr
        eqf = jnp.where(eq, 1.0, 0.0)
        prefix = _dot(eqf.astype(BF16), tri_ref[...]) + eq_seen
        sel = (keys > thr) | (eq & (prefix <= need))
        madd = jnp.where(sel, 0.0, MASKED)
        for p in range(N_HEADS // 2):
            cols = slice(p * LANES, (p + 1) * LANES)
            s2 = _dot_nt(qz_ref[p], k_ref[pl.ds(kbase, KC), cols])
            pb = []
            alphas = []
            for hh in range(2):
                h = 2 * p + hh
                if near:
                    tiles = []
                    for s in range(SUB):
                        dblk = qb - (c * SUB + s)
                        idx = jnp.where(dblk < 0, 3, jnp.minimum(dblk, 2))
                        tiles.append(tabs_ref[h, idx])
                    bias = jnp.concatenate(tiles, axis=1)
                else:
                    bias = rb_ref[N_BUCKETS - 1, h]
                s = s2[hh * QB:(hh + 1) * QB] + madd + bias
                m_old = m_ref[h]
                m_new = jnp.maximum(m_old, jnp.max(s, axis=1, keepdims=True))
                pr = jnp.exp(s - jnp.concatenate([m_new] * SUB, axis=1))
                alpha = jnp.exp(m_old - m_new)
                l_ref[h] = alpha * l_ref[h] + jnp.sum(pr, axis=1, keepdims=True)
                m_ref[h] = m_new
                pb.append(pr.astype(BF16))
                alphas.append(alpha)
            pv = _dot(jnp.concatenate(pb, axis=0), v_ref[pl.ds(kbase, KC), cols])
            a_old = acc_ref[:, cols]
            acc_ref[:, cols] = _lane_half_select(
                (QB, LANES), alphas[0] * a_old + pv[:QB], alphas[1] * a_old + pv[QB:])
        return eq_seen + jnp.sum(eqf, axis=1, keepdims=True)

    n_far = jnp.maximum(qb - (SUB + 1), -SUB) // SUB + 1
    eq_seen = lax.fori_loop(0, n_far, functools.partial(attend_chunk, near=False), jnp.zeros((QB, 1), F32))
    lax.fori_loop(n_far, nch, functools.partial(attend_chunk, near=True), eq_seen)

    for p in range(N_HEADS // 2):
        cols = slice(p * LANES, (p + 1) * LANES)
        o_ref[:, cols] = acc_ref[:, cols] / _lane_half_select((QB, LANES), l_ref[2 * p], l_ref[2 * p + 1])


def dsa_prompt(rel_bias, qi, wis, ki2, qs, kb, vb, tabs, tri, batch, seq):
    QB = LANES
    nqb = seq // QB
    n_sel = min(TOPK_MAX, seq // 4)
    tok = lambda w: pl.BlockSpec((QB, w), lambda b, i: (b * nqb + i, 0))
    per_batch = lambda w: _resident((seq, w), lambda b, i: (b, 0))
    nchunks = seq // KEY_CHUNK
    return pl.pallas_call(
        functools.partial(_dsa_prompt_kernel, seq=seq, n_sel=n_sel),
        grid=(batch, nqb),
        in_specs=[pl.BlockSpec(memory_space=pltpu.SMEM),
                  tok(ATT_WIDTH), tok(IDX_HEADS), per_batch(LANES), tok(ATT_WIDTH),
                  per_batch(ATT_WIDTH), per_batch(ATT_WIDTH),
                  _resident(tabs.shape, lambda b, i: (0, 0, 0, 0)),
                  _resident(tri.shape, lambda b, i: (0, 0))],
        out_specs=tok(ATT_WIDTH),
        out_shape=jax.ShapeDtypeStruct((batch * seq, ATT_WIDTH), F32),
        scratch_shapes=[pltpu.VMEM((nchunks, QB, KEY_CHUNK), I32),
                        pltpu.VMEM((N_HEADS // 2, 2 * QB, LANES), BF16),
                        pltpu.VMEM((N_HEADS // 2, 2 * QB, LANES), BF16),
                        pltpu.VMEM((IDX_HEADS, QB, LANES), F32),
                        pltpu.VMEM((N_HEADS, QB, LANES), F32),
                        pltpu.VMEM((N_HEADS, QB, LANES), F32),
                        pltpu.VMEM((QB, ATT_WIDTH), F32)],
        compiler_params=_params(("arbitrary", "arbitrary")),
        name="dsa_prompt",
    )(rel_bias, qi, wis, ki2, qs, kb, vb, tabs, tri)


PAGES_PER_STEP = 8


def _page_spec(shape_tail, layer, i):
    def index_map(b, g, pt):
        return (layer, pt[b, g * PAGES_PER_STEP + i]) + (0,) * len(shape_tail)
    return pl.BlockSpec((None, None) + shape_tail, index_map)


def _sample_scores_kernel(pt_ref, qi_ref, wb_ref, *rest):
    page_refs = rest[:PAGES_PER_STEP]
    o_ref = rest[PAGES_PER_STEP]
    qi = qi_ref[0]
    wb = wb_ref[0]
    for i in range(PAGES_PER_STEP):
        d = _dot_nt(qi, page_refs[i][...].astype(BF16))
        w = jnp.maximum(d, 0.0) * wb
        o_ref[0, :, i * LANES:(i + 1) * LANES] = jnp.sum(w.reshape(IDX_HEADS, 8, LANES), axis=0)


def sample_scores(page_table, qi_rows, wb_rows, cache_kidx, layer):
    bd, n_pages = page_table.shape
    past = n_pages * PAGE_SIZE
    per_seq = lambda a: pl.BlockSpec((1,) + a.shape[1:], lambda b, g, pt: (b, 0, 0))
    grid_spec = pltpu.PrefetchScalarGridSpec(
        num_scalar_prefetch=1,
        grid=(bd, n_pages // PAGES_PER_STEP),
        in_specs=[per_seq(qi_rows), per_seq(wb_rows)]
                 + [_page_spec((PAGE_SIZE, IDX_DIM), layer, i) for i in range(PAGES_PER_STEP)],
        out_specs=pl.BlockSpec((1, 8, PAGES_PER_STEP * LANES), lambda b, g, pt: (b, 0, g)))
    return pl.pallas_call(
        _sample_scores_kernel,
        grid_spec=grid_spec,
        out_shape=jax.ShapeDtypeStruct((bd, 8, past), F32),
        compiler_params=_params(("arbitrary", "arbitrary")),
        name="sample_scores",
    )(page_table, qi_rows, wb_rows, *([cache_kidx] * PAGES_PER_STEP))


def _sample_threshold_kernel(sc_ref, qi_ref, wb_ref, kin_ref, thr_ref, need_ref, knew_ref, keys_ref,
                             *, past, n_new, n_sel):
    d = _dot_nt(qi_ref[0], kin_ref[0].astype(BF16))
    s_new = jnp.sum((jnp.maximum(d, 0.0) * wb_ref[0]).reshape(IDX_HEADS, 8, LANES), axis=0)
    t = lax.broadcasted_iota(I32, (8, LANES), 0)
    j = lax.broadcasted_iota(I32, (8, LANES), 1)
    k_new = _ordered_keys(jnp.where(j <= t, s_new, NEG_INF))
    k_new = jnp.where(j < n_new, k_new, INT_MIN)
    knew_ref[0] = k_new
    keys_ref[:, :past] = _ordered_keys(sc_ref[0])
    keys_ref[:, past:] = k_new

    def count_ge(cand):
        return jnp.sum(jnp.where(keys_ref[...] >= cand, 1.0, 0.0), axis=1, keepdims=True)

    kf = float(n_sel)
    t0 = jnp.where(count_ge(jnp.zeros((8, 1), I32)) >= kf, 0, INT_MIN).astype(I32)

    def bit_step(i, tcur):
        cand = tcur + lax.shift_left(jnp.int32(1), 30 - i)
        return jnp.where(count_ge(cand) >= kf, cand, tcur)

    thr = lax.fori_loop(0, 31, bit_step, t0)
    thr_ref[0] = jnp.broadcast_to(thr, (8, LANES))
    need_ref[0] = jnp.broadcast_to(kf - count_ge(thr + 1), (8, LANES))


def sample_threshold(scores, qi_rows, wb_rows, ki_new_pad, n_new):
    bd, _, past = scores.shape
    n_sel = min(TOPK_MAX, (past + n_new) // 4)
    per_seq = lambda a: pl.BlockSpec((1,) + a.shape[1:], lambda b: (b, 0, 0))
    small = jax.ShapeDtypeStruct((bd, 8, LANES), F32)
    return pl.pallas_call(
        functools.partial(_sample_threshold_kernel, past=past, n_new=n_new, n_sel=n_sel),
        grid=(bd,),
        in_specs=[per_seq(scores), per_seq(qi_rows), per_seq(wb_rows), per_seq(ki_new_pad)],
        out_specs=[pl.BlockSpec((1, 8, LANES), lambda b: (b, 0, 0))] * 3,
        out_shape=[jax.ShapeDtypeStruct((bd, 8, LANES), I32), small, jax.ShapeDtypeStruct((bd, 8, LANES), I32)],
        scratch_shapes=[pltpu.VMEM((8, past + LANES), I32)],
        compiler_params=_params(("arbitrary",)),
        name="sample_threshold",
    )(scores, qi_rows, wb_rows, ki_new_pad)


def _sample_attend_kernel(pt_ref, qbd_ref, sc_ref, thr_ref, need_ref, knew_ref, kn_ref, vn_ref, stab_ref,
                          tri_ref, *rest):
    P = PAGES_PER_STEP
    k_refs = rest[:P]
    v_refs = rest[P:2 * P]
    o_ref = rest[2 * P]
    m_ref, l_ref, acc_ref, eq_ref = rest[2 * P + 1:]
    g = pl.program_id(1)
    n_groups = pl.num_programs(1)

    @pl.when(g == 0)
    def _():
        m_ref[...] = jnp.full(m_ref.shape, NEG_INF, F32)
        l_ref[...] = jnp.zeros(l_ref.shape, F32)
        acc_ref[...] = jnp.zeros(acc_ref.shape, F32)
        eq_ref[...] = jnp.zeros(eq_ref.shape, F32)

    qbd = qbd_ref[0]
    thr = thr_ref[0]
    need = need_ref[0]

    def attend(keys, kmat, vmat, bias):
        eq = keys == thr
        eqf = jnp.where(eq, 1.0, 0.0)
        prefix = _dot(eqf.astype(BF16), tri_ref[...]) + eq_ref[...]
        sel = (keys > thr) | (eq & (prefix <= need))
        eq_ref[...] = eq_ref[...] + jnp.sum(eqf, axis=1, keepdims=True)
        madd = jnp.where(sel, 0.0, MASKED)
        s = _dot_nt(qbd, kmat) + jnp.concatenate([madd] * N_HEADS, axis=0) + bias
        m_old = m_ref[...]
        m_new = jnp.maximum(m_old, jnp.max(s, axis=1, keepdims=True))
        pr = jnp.exp(s - m_new)
        alpha = jnp.exp(m_old - m_new)
        l_ref[...] = alpha * l_ref[...] + jnp.sum(pr, axis=1, keepdims=True)
        m_ref[...] = m_new
        acc_ref[...] = jnp.concatenate([alpha] * (ATT_WIDTH // LANES), axis=1) * acc_ref[...] \
            + _dot(pr.astype(BF16), vmat)

    for i in range(P):
        is_last = (g == n_groups - 1) if i == P - 1 else False
        bias = jnp.where(is_last, stab_ref[1], stab_ref[0]) if i == P - 1 else stab_ref[0]
        keys = _ordered_keys(sc_ref[0, :, i * LANES:(i + 1) * LANES])
        attend(keys, k_refs[i][...].astype(BF16), v_refs[i][...].astype(BF16), bias)

    @pl.when(g == n_groups - 1)
    def _():
        attend(knew_ref[0], kn_ref[0].astype(BF16), vn_ref[0].astype(BF16), stab_ref[2])
        o64 = acc_ref[...] / jnp.concatenate([l_ref[...]] * (ATT_WIDTH // LANES), axis=1)
        head_of_col = lax.broadcasted_iota(I32, (8, ATT_WIDTH), 1) // HEAD_DIM
        out = jnp.zeros((8, ATT_WIDTH), F32)
        for h in range(N_HEADS):
            out = jnp.where(head_of_col == h, o64[h * 8:(h + 1) * 8], out)
        o_ref[0] = out


def sample_attend(page_table, qbd, scores, thr, need, keys_new, k_new_pad, v_new_pad, stab, tri,
                  cache_k, cache_v, layer):
    bd, n_pages = page_table.shape
    P = PAGES_PER_STEP
    per_seq = lambda a: pl.BlockSpec((1,) + a.shape[1:], lambda b, g, pt: (b, 0, 0))
    const = lambda a: pl.BlockSpec(a.shape, lambda b, g, pt: (0,) * a.ndim)
    grid_spec = pltpu.PrefetchScalarGridSpec(
        num_scalar_prefetch=1,
        grid=(bd, n_pages // P),
        in_specs=[per_seq(qbd),
                  pl.BlockSpec((1, 8, P * LANES), lambda b, g, pt: (b, 0, g)),
                  per_seq(thr), per_seq(need), per_seq(keys_new), per_seq(k_new_pad), per_seq(v_new_pad),
                  const(stab), const(tri)]
                 + [_page_spec((PAGE_SIZE, ATT_WIDTH), layer, i) for i in range(P)]
                 + [_page_spec((PAGE_SIZE, ATT_WIDTH), layer, i) for i in range(P)],
        out_specs=pl.BlockSpec((1, 8, ATT_WIDTH), lambda b, g, pt: (b, 0, 0)),
        scratch_shapes=[pltpu.VMEM((HALF, LANES), F32), pltpu.VMEM((HALF, LANES), F32),
                        pltpu.VMEM((HALF, ATT_WIDTH), F32), pltpu.VMEM((8, LANES), F32)])
    return pl.pallas_call(
        _sample_attend_kernel,
        grid_spec=grid_spec,
        out_shape=jax.ShapeDtypeStruct((bd, 8, ATT_WIDTH), F32),
        compiler_params=_params(("arbitrary", "arbitrary")),
        name="sample_attend",
    )(page_table, qbd, scores, thr, need, keys_new, k_new_pad, v_new_pad, stab, tri,
      *([cache_k] * P), *([cache_v] * P))


def _mixer_out_kernel(att_ref, sgu_ref, x_ref, g1_ref, sh2_ref, sc2_ref, woa_ref, wob_ref, lng_ref, lnb_ref,
                      wq_ref, x1_ref, h2_ref, qp_ref):
    mix = _dot(att_ref[...].astype(BF16), woa_ref[...]) + _dot(sgu_ref[...], wob_ref[...])
    x1 = _layer_norm(DEEPNORM_ALPHA * x_ref[...] + g1_ref[0] * mix) * lng_ref[...] + lnb_ref[...]
    x1_ref[...] = x1
    h2 = (_layer_norm(x1) * (1.0 + sc2_ref[0]) + sh2_ref[0]).astype(BF16)
    h2_ref[...] = h2
    qp_ref[...] = _dot(h2, wq_ref[...]).astype(BF16)


def mixer_out(att, sgu, x, g1, sh2, sc2, cond_map, cond_rows, wts, tb):
    n, d = x.shape
    tok = lambda w: pl.BlockSpec((tb, w), lambda i: (i, 0))
    full = lambda a: _resident(a.shape, lambda i: (0,) * a.ndim)
    cond = pl.BlockSpec((1, cond_rows, d), cond_map)
    warrs = [wts[nm] for nm in ("woa", "wob", "ln_mix_g", "ln_mix_b", "peer_wq")]
    nq = PEER_HEADS * D_KEY
    return pl.pallas_call(
        _mixer_out_kernel,
        grid=(n // tb,),
        in_specs=[tok(ATT_WIDTH), tok(GMLP_WIDTH), tok(d), cond, cond, cond] + [full(a) for a in warrs],
        out_specs=[tok(d), tok(d), tok(nq)],
        out_shape=[jax.ShapeDtypeStruct((n, d), F32), jax.ShapeDtypeStruct((n, d), BF16),
                   jax.ShapeDtypeStruct((n, nq), BF16)],
        compiler_params=_params(("arbitrary",)),
        name="mixer_out",
    )(att, sgu, x, g1, sh2, sc2, *warrs)


def _top16_columns(s):
    rows, tt = s.shape
    ridx = lax.broadcasted_iota(I32, (rows, tt), 0)
    r16 = lax.broadcasted_iota(I32, (PEER_TOPK, tt), 0)

    def step(r, carry):
        cur, rank, vals = carry
        m = jnp.max(cur, axis=0, keepdims=True)
        first = jnp.min(jnp.where(cur == m, ridx, rows), axis=0, keepdims=True)
        hit = ridx == first
        return (jnp.where(hit, -jnp.inf, cur), jnp.where(hit, r, rank), jnp.where(r16 == r, m, vals))

    init = (s, jnp.full((rows, tt), PEER_TOPK, I32), jnp.zeros((PEER_TOPK, tt), F32))
    _, rank, vals = lax.fori_loop(0, PEER_TOPK, step, init)
    return vals, rank


def _peer_select_kernel(qp_ref, sub_ref, ea_ref, cnt_ref, eb_ref, rk_ref):
    tt = qp_ref.shape[0]
    for h in range(PEER_HEADS):
        s0 = _dot_nt(sub_ref[0], qp_ref[:, (2 * h) * LANES:(2 * h + 1) * LANES])
        s1 = _dot_nt(sub_ref[1], qp_ref[:, (2 * h + 1) * LANES:(2 * h + 2) * LANES])
        v0, rank0 = _top16_columns(s0)
        v1, rank1 = _top16_columns(s1)
        cand = jnp.concatenate([v0[r:r + 1] + v1 for r in range(PEER_TOPK)], axis=0)
        _, crank = _top16_columns(cand)
        sel = jnp.where(crank < PEER_TOPK, 1.0, 0.0)
        e0 = jnp.exp(v0 - v0[0:1])
        e1 = jnp.exp(v1 - v1[0:1])
        z = jnp.zeros((1, tt), F32)
        cnts = []
        for r in range(PEER_TOPK):
            blk = sel[r * PEER_TOPK:(r + 1) * PEER_TOPK]
            cnts.append(jnp.sum(blk, axis=0, keepdims=True))
            z = z + e0[r:r + 1] * jnp.sum(blk * e1, axis=0, keepdims=True)
        cnt_i = jnp.zeros((N_KEYS, tt), F32)
        for r in range(PEER_TOPK):
            cnt_i = jnp.where(rank0 == r, cnts[r], cnt_i)
        ea_ref[h] = jnp.where(rank0 < PEER_TOPK, jnp.exp(s0 - v0[0:1]), 0.0) / z
        cnt_ref[h] = cnt_i
        eb_ref[h] = jnp.where(rank1 < PEER_TOPK, jnp.exp(s1 - v1[0:1]), 0.0)
        rk_ref[h] = rank1.astype(F32)


def peer_select(qp, subkeys_b, tbs):
    n = qp.shape[0]
    out = jax.ShapeDtypeStruct((PEER_HEADS, N_KEYS, n), F32)
    ospec = pl.BlockSpec((PEER_HEADS, N_KEYS, tbs), lambda i: (0, 0, i))
    return pl.pallas_call(
        _peer_select_kernel,
        grid=(n // tbs,),
        in_specs=[pl.BlockSpec((tbs, qp.shape[1]), lambda i: (i, 0)),
                  _resident(subkeys_b.shape, lambda i: (0, 0, 0))],
        out_specs=[ospec] * 4,
        out_shape=[out] * 4,
        compiler_params=_params(("arbitrary",)),
        name="peer_select",
    )(qp, subkeys_b)


def _peer_experts_kernel(h2_ref, u_ref, vt_ref, ea_ref, cnt_ref, eb_ref, rk_ref, x1_ref, g2_ref,
                         lng_ref, lnb_ref, o_ref, acc_ref, *, eb):
    j = pl.program_id(1)
    tb = h2_ref.shape[0]

    @pl.when(j == 0)
    def _():
        acc_ref[...] = jnp.zeros(acc_ref.shape, F32)

    act = jax.nn.gelu(_dot_nt(u_ref[...], h2_ref[...]))
    gates = []
    for ii in range(eb // N_KEYS):
        i = j * (eb // N_KEYS) + ii
        w = jnp.zeros((N_KEYS, tb), F32)
        for h in range(PEER_HEADS):
            cnt = cnt_ref[h, pl.ds(i, 1), :]
            ea = ea_ref[h, pl.ds(i, 1), :]
            w = w + jnp.where(rk_ref[h] < cnt, ea * eb_ref[h], 0.0)
        gates.append(w)
    hmat = (jnp.concatenate(gates, axis=0) * act).astype(BF16)
    acc_ref[...] += _dot(vt_ref[...], hmat)

    @pl.when(j == pl.num_programs(1) - 1)
    def _():
        ffn = acc_ref[...].T
        y = DEEPNORM_ALPHA * x1_ref[...] + g2_ref[0] * ffn
        o_ref[...] = _layer_norm(y) * lng_ref[...] + lnb_ref[...]


def peer_experts(h2, u_b, vt_b, sel, x1, g2, cond_map, cond_rows, lng, lnb, tb, eb):
    n, d = x1.shape
    ne = u_b.shape[0]
    cm = lambda i, j: cond_map(i)
    selspec = pl.BlockSpec((PEER_HEADS, N_KEYS, tb), lambda i, j: (0, 0, i))
    vec = lambda a: _resident(a.shape, lambda i, j: (0, 0))
    return pl.pallas_call(
        functools.partial(_peer_experts_kernel, eb=eb),
        grid=(n // tb, ne // eb),
        in_specs=[pl.BlockSpec((tb, d), lambda i, j: (i, 0)),
                  pl.BlockSpec((eb, d), lambda i, j: (j, 0)),
                  pl.BlockSpec((d, eb), lambda i, j: (0, j)),
                  selspec, selspec, selspec, selspec,
                  pl.BlockSpec((tb, d), lambda i, j: (i, 0)),
                  pl.BlockSpec((1, cond_rows, d), cm), vec(lng), vec(lnb)],
        out_specs=pl.BlockSpec((tb, d), lambda i, j: (i, 0)),
        out_shape=jax.ShapeDtypeStruct((n, d), F32),
        scratch_shapes=[pltpu.VMEM((d, tb), F32)],
        compiler_params=_params(("arbitrary", "arbitrary")),
        name="peer_experts",
    )(h2, u_b, vt_b, *sel, x1, g2, lng, lnb)


def _layer_weights(l, w_in, gmlp_ln_g, gmlp_ln_b, w_s, b_s, w_out, ln_mix_g, ln_mix_b, peer_wq, peer_subkeys,
                   peer_u, peer_v, ln_ffn_g, ln_ffn_b, dec_seq, dec_batch):
    a = ATT_WIDTH
    o_qi = 3 * a
    o_ki = o_qi + IDX_HEADS * IDX_DIM
    o_wi = o_ki + IDX_DIM
    o_u = o_wi + IDX_HEADS
    o_vg = o_u + GMLP_WIDTH
    w = w_in[l]
    wkw = jnp.pad(w[:, o_ki:o_u], ((0, 0), (0, LANES - IDX_DIM - IDX_HEADS)))
    row = lambda v: v.reshape(1, -1)
    tril = w_s[l] * jnp.tril(jnp.ones((CHUNK, CHUNK), F32))

    def pair_bias(b):
        bb = jnp.broadcast_to(b[:, :, None], b.shape + (HALF,))
        return jnp.concatenate([bb[0::2], bb[1::2]], axis=-1)

    ws_dec = jnp.einsum("ab,gij->gaibj", jnp.eye(dec_batch, dtype=F32), tril[:, :dec_seq, :dec_seq])
    ws_dec = ws_dec.reshape(GMLP_GROUPS, dec_batch * dec_seq, dec_batch * dec_seq)
    return {
        "wq": w[:, :a].astype(BF16), "wk": w[:, a:2 * a].astype(BF16), "wv": w[:, 2 * a:3 * a].astype(BF16),
        "wqi": w[:, o_qi:o_ki].astype(BF16), "wkw": wkw.astype(BF16),
        "wu": w[:, o_u:o_vg].astype(BF16), "wvg": w[:, o_vg:].astype(BF16),
        "lng": row(gmlp_ln_g[l]), "lnb": row(gmlp_ln_b[l]),
        "ws": tril.astype(BF16), "bs": pair_bias(b_s[l]),
        "ws_dec": ws_dec.astype(BF16), "bs_dec": pair_bias(jnp.tile(b_s[l][:, :dec_seq], (1, dec_batch))),
        "woa": w_out[l][:a].astype(BF16), "wob": w_out[l][a:].astype(BF16),
        "ln_mix_g": row(ln_mix_g[l]), "ln_mix_b": row(ln_mix_b[l]),
        "peer_wq": peer_wq[l].astype(BF16), "subkeys": peer_subkeys[l].astype(BF16),
        "u": peer_u[l].astype(BF16), "vt": peer_v[l].T.astype(BF16),
        "ln_ffn_g": row(ln_ffn_g[l]), "ln_ffn_b": row(ln_ffn_b[l]),
    }


def kernel(x_prompt, x_sample, cache_k, cache_v, cache_kidx, page_table, c_prompt, c_sample, rel_bias, w_ada, b_ada, w_in, gmlp_ln_g, gmlp_ln_b, w_s, b_s, w_out, ln_mix_g, ln_mix_b, peer_wq, peer_subkeys, peer_u, peer_v, ln_ffn_g, ln_ffn_b):
    batch, seq, d = x_prompt.shape
    bd, tn, _ = x_sample.shape
    n_p = batch * seq
    n_s = bd * tn
    n_pool = cache_k.shape[1]
    assert seq % KEY_CHUNK == 0 and tn == 8 and n_s % LANES == 0

    g_all = batch + bd
    g_pad = -(-g_all // 8) * 8
    c_all = jnp.pad(jnp.concatenate([c_prompt, c_sample], axis=0), ((0, g_pad - g_all), (0, 0)))
    ada = ada_modulation(c_all, w_ada, b_ada)
    tabs, stab = bias_tables(rel_bias)
    tri_chunk = jnp.triu(jnp.ones((KEY_CHUNK, KEY_CHUNK), BF16))
    tri_page = jnp.triu(jnp.ones((LANES, LANES), BF16))

    ck = cache_k.reshape(DEPTH, n_pool, PAGE_SIZE, ATT_WIDTH)
    cv = cache_v.reshape(DEPTH, n_pool, PAGE_SIZE, ATT_WIDTH)

    tb_p = 512
    prompt_map = lambda i: (i // (seq // tb_p), 0, 0)
    sample_map = lambda i: (0, i, 0)

    xp = x_prompt.reshape(n_p, d)
    xs = x_sample.reshape(n_s, d)
    outs = {k: [] for k in ("kp", "vp", "kip", "ks", "vs", "kis", "vgs")}
    for l in range(DEPTH):
        wts = _layer_weights(l, w_in, gmlp_ln_g, gmlp_ln_b, w_s, b_s, w_out, ln_mix_g, ln_mix_b, peer_wq,
                             peer_subkeys, peer_u, peer_v, ln_ffn_g, ln_ffn_b, tn, bd)
        mods = jnp.split(ada[l], 6, axis=-1)
        mod_p = [m[:batch].reshape(batch, 1, d) for m in mods]
        mod_s = [jnp.repeat(m[batch:g_all], tn, axis=0).reshape(1, n_s, d) for m in mods]

        sh1, sc1, g1, sh2, sc2, g2 = mod_p
        qs, k, v, kb, vb, qi, ki, ki2, wis, sgu, _ = mixer_in(xp, sh1, sc1, prompt_map, 1, wts, tb_p, CHUNK)
        att = dsa_prompt(rel_bias, qi, wis, ki2, qs, kb, vb, tabs, tri_chunk, batch, seq)
        x1, h2, qp = mixer_out(att, sgu, xp, g1, sh2, sc2, prompt_map, 1, wts, tb_p)
        sel = peer_select(qp, wts["subkeys"], LANES)
        xp = peer_experts(h2, wts["u"], wts["vt"], sel, x1, g2, prompt_map, 1,
                          wts["ln_ffn_g"], wts["ln_ffn_b"], tb_p, 512)
        outs["kp"].append(k.reshape(batch, seq, N_HEADS, HEAD_DIM))
        outs["vp"].append(v.reshape(batch, seq, N_HEADS, HEAD_DIM))
        outs["kip"].append(ki.reshape(batch, seq, IDX_DIM))

        sh1, sc1, g1, sh2, sc2, g2 = mod_s
        wts_s = dict(wts, ws=wts["ws_dec"], bs=wts["bs_dec"])
        qs, k, v, kb, vb, qi, ki, ki2, wis, sgu, vg = mixer_in(xs, sh1, sc1, sample_map, n_s, wts_s, n_s, n_s)
        qi_rows = qi.reshape(bd, tn, IDX_HEADS, IDX_DIM).transpose(0, 2, 1, 3).reshape(bd, HALF, IDX_DIM)
        wb_rows = jnp.broadcast_to(wis.reshape(bd, tn, IDX_HEADS).transpose(0, 2, 1).reshape(bd, HALF, 1),
                                   (bd, HALF, LANES))
        head_mask = (jnp.arange(ATT_WIDTH)[None, :] // HEAD_DIM == jnp.arange(HALF)[:, None] // tn)
        qbd = jnp.where(head_mask[None], jnp.tile(qs.reshape(bd, 1, tn, ATT_WIDTH), (1, N_HEADS, 1, 1))
                        .reshape(bd, HALF, ATT_WIDTH), 0).astype(BF16)
        pad_rows = lambda a_: jnp.pad(a_.reshape(bd, tn, -1), ((0, 0), (0, LANES - tn), (0, 0)))
        scores = sample_scores(page_table, qi_rows, wb_rows, cache_kidx, l)
        thr, need, keys_new = sample_threshold(scores, qi_rows, wb_rows, pad_rows(ki), tn)
        att = sample_attend(page_table, qbd, scores, thr, need, keys_new, pad_rows(k), pad_rows(v), stab,
                            tri_page, ck, cv, l).reshape(n_s, ATT_WIDTH)
        x1, h2, qp = mixer_out(att, sgu, xs, g1, sh2, sc2, sample_map, n_s, wts, n_s)
        sel = peer_select(qp, wts["subkeys"], LANES)
        xs = peer_experts(h2, wts["u"], wts["vt"], sel, x1, g2, sample_map, n_s,
                          wts["ln_ffn_g"], wts["ln_ffn_b"], n_s, 512)
        outs["ks"].append(k.reshape(bd, tn, N_HEADS, HEAD_DIM))
        outs["vs"].append(v.reshape(bd, tn, N_HEADS, HEAD_DIM))
        outs["kis"].append(ki.reshape(bd, tn, IDX_DIM))
        outs["vgs"].append(vg.reshape(bd, tn, GMLP_GROUPS, GMLP_WIDTH // GMLP_GROUPS))

    st = lambda name: jnp.stack(outs[name])
    return (xp.reshape(batch, seq, d), xs.reshape(bd, tn, d), st("kp"), st("vp"), st("kip"),
            st("ks"), st("vs"), st("kis"), st("vgs"))
```
